```python
import math
import jax
import jax.numpy as jnp
from jax import lax
import numpy as np

D_MODEL = 1024
BATCH = 8
SEQ = 4096
DEPTH = 4

N_MIXERS = 2
N_EVEN_LAYERS = (DEPTH + 1) // 2
N_ODD_LAYERS = DEPTH // 2

MLSTM_HEADS = 4
MLSTM_DV = D_MODEL // MLSTM_HEADS
MLSTM_DQK = MLSTM_DV // 2
MLSTM_QK_WIDTH = 2 * MLSTM_HEADS * MLSTM_DQK
MLSTM_V_WIDTH = MLSTM_HEADS * MLSTM_DV
MLSTM_PROJ = MLSTM_QK_WIDTH + 2 * MLSTM_V_WIDTH + 2 * MLSTM_HEADS
MLSTM_CHUNK = 64
CONV_WIDTH = 4
FGATE_BIAS_LO = 3.0
FGATE_BIAS_HI = 6.0

DIFF_HEADS = 8
DIFF_HEAD_DIM = D_MODEL // (2 * DIFF_HEADS)
DIFF_V_DIM = 2 * DIFF_HEAD_DIM
DIFF_QK_WIDTH = 2 * DIFF_HEADS * DIFF_HEAD_DIM
DIFF_PROJ = 2 * DIFF_QK_WIDTH + DIFF_HEADS * DIFF_V_DIM
Q_BLOCK = 128

NUM_BUCKETS = 32
REL_MAX_DISTANCE = 128

D_FF = 3584
N_EXPERTS = 8
TOP_K = 2
MOE_BLOCK = 256

NORM_EPS = 1e-6
SUBLN_EPS = 1e-5

kernel_name = 'mlstm_diffattn_moe_hybrid'


def rms_norm(x, w, eps=NORM_EPS):
    xf = x.astype(jnp.float32)
    y = xf * lax.rsqrt(jnp.mean(xf * xf, axis=-1, keepdims=True) + eps)
    return (y * w.astype(jnp.float32)).astype(x.dtype)


def swiglu(h, w_gate, w_up, w_down):
    return (jax.nn.silu(h @ w_gate) * (h @ w_up)) @ w_down


def causal_depthwise_conv(x, w, b):
    c = x.shape[-1]
    y = lax.conv_general_dilated(
        x, w.reshape(CONV_WIDTH, 1, c).astype(x.dtype), window_strides=(1,),
        padding=[(CONV_WIDTH - 1, 0)], dimension_numbers=('NWC', 'WIO', 'NWC'),
        feature_group_count=c)
    return y + b.astype(x.dtype)


def _mlstm_chunk_step(carry, inp):
    c_mem, n_mem, m_prev = carry
    q, k, v, log_i, log_f = inp
    L = q.shape[2]
    b = jnp.cumsum(log_f, axis=-1)
    causal = jnp.tril(jnp.ones((L, L), dtype=bool))
    log_d = jnp.where(causal, b[..., :, None] - b[..., None, :] + log_i[..., None, :], -jnp.inf)
    log_inter = b + m_prev[..., None]
    m_t = jnp.maximum(log_inter, jnp.max(log_d, axis=-1))
    s = jnp.einsum('bhtd,bhsd->bhts', q, k) * jnp.exp(log_d - m_t[..., None])
    inter = jnp.exp(log_inter - m_t)
    num = jnp.einsum('bhts,bhsv->bhtv', s, v) + inter[..., None] * jnp.einsum('bhtd,bhdv->bhtv', q, c_mem)
    den = jnp.sum(s, axis=-1) + inter * jnp.einsum('bhtd,bhd->bht', q, n_mem)
    h = num / jnp.maximum(jnp.abs(den), jnp.exp(-m_t))[..., None]
    g = b[..., -1]
    log_w = g[..., None] - b + log_i
    m_new = jnp.maximum(g + m_prev, jnp.max(log_w, axis=-1))
    kw = k * jnp.exp(log_w - m_new[..., None])[..., None]
    decay = jnp.exp(g + m_prev - m_new)
    c_new = decay[..., None, None] * c_mem + jnp.einsum('bhsd,bhsv->bhdv', kw, v)
    n_new = decay[..., None] * n_mem + jnp.sum(kw, axis=2)
    return (c_new, n_new, m_new), h


def mlstm_chunkwise(q, k, v, log_i, log_f):
    B, S, H, DK = q.shape
    DV = v.shape[-1]
    L = MLSTM_CHUNK
    nc = S // L

    def to_chunks(t):
        return t.reshape(B, nc, L, H, t.shape[-1]).transpose(1, 0, 3, 2, 4)

    def gate_chunks(t):
        return t.reshape(B, nc, L, H).transpose(1, 0, 3, 2)

    init = (jnp.zeros((B, H, DK, DV), jnp.float32), jnp.zeros((B, H, DK), jnp.float32),
            jnp.zeros((B, H), jnp.float32))
    _, hs = lax.scan(_mlstm_chunk_step, init,
                     (to_chunks(q), to_chunks(k), to_chunks(v), gate_chunks(log_i), gate_chunks(log_f)))
    return hs.transpose(1, 0, 3, 2, 4).reshape(B, S, H, DV)


def mlstm_mixer(h, w_in, conv_w, conv_b, b_igate, b_fgate, head_norm, w_out):
    B, S, _ = h.shape
    proj = h @ w_in
    o1 = MLSTM_QK_WIDTH
    o2 = o1 + MLSTM_V_WIDTH
    o3 = o2 + MLSTM_V_WIDTH
    o4 = o3 + MLSTM_HEADS
    qk = jax.nn.silu(causal_depthwise_conv(proj[..., :o1], conv_w, conv_b))
    q = qk[..., :o1 // 2].reshape(B, S, MLSTM_HEADS, MLSTM_DQK).astype(jnp.float32)
    k = qk[..., o1 // 2:].reshape(B, S, MLSTM_HEADS, MLSTM_DQK).astype(jnp.float32) * (MLSTM_DQK ** -0.5)
    v = proj[..., o1:o2].reshape(B, S, MLSTM_HEADS, MLSTM_DV).astype(jnp.float32)
    o_gate = jax.nn.sigmoid(proj[..., o2:o3])
    log_i = (proj[..., o3:o4] + b_igate).astype(jnp.float32)
    log_f = jax.nn.log_sigmoid((proj[..., o4:] + b_fgate).astype(jnp.float32))
    hh = mlstm_chunkwise(q, k, v, log_i, log_f)
    hh = rms_norm(hh, head_norm.reshape(MLSTM_HEADS, MLSTM_DV)).astype(h.dtype)
    return (o_gate * hh.reshape(B, S, MLSTM_V_WIDTH)) @ w_out


def t5_causal_bucket(rel):
    n = jnp.maximum(rel, 0)
    max_exact = NUM_BUCKETS // 2
    nf = jnp.maximum(n, 1).astype(jnp.float32)
    large = max_exact + (jnp.log(nf / max_exact) / math.log(REL_MAX_DISTANCE / max_exact)
                         * (NUM_BUCKETS - max_exact)).astype(jnp.int32)
    large = jnp.minimum(large, NUM_BUCKETS - 1)
    return jnp.where(n < max_exact, n, large)


def diff_attention(h, w_in, lam_q1, lam_k1, lam_q2, lam_k2, subln, w_out, rel_table, layer_idx):
    B, S, _ = h.shape
    proj = h @ w_in
    q = proj[..., :DIFF_QK_WIDTH].reshape(B, S, 2 * DIFF_HEADS, DIFF_HEAD_DIM).transpose(0, 2, 1, 3)
    k = proj[..., DIFF_QK_WIDTH:2 * DIFF_QK_WIDTH].reshape(B, S, 2 * DIFF_HEADS, DIFF_HEAD_DIM).transpose(0, 2, 1, 3)
    v = proj[..., 2 * DIFF_QK_WIDTH:].reshape(B, S, DIFF_HEADS, DIFF_V_DIM).transpose(0, 2, 1, 3)
    lam_init = 0.8 - 0.6 * math.exp(-0.3 * layer_idx)
    lam = (jnp.exp(jnp.sum(lam_q1.astype(jnp.float32) * lam_k1.astype(jnp.float32)))
           - jnp.exp(jnp.sum(lam_q2.astype(jnp.float32) * lam_k2.astype(jnp.float32))) + lam_init)
    scale = DIFF_HEAD_DIM ** -0.5
    outs = []
    for j in range(S // Q_BLOCK):
        q0 = j * Q_BLOCK
        kv_len = q0 + Q_BLOCK
        qpos = q0 + jnp.arange(Q_BLOCK, dtype=jnp.int32)
        kpos = jnp.arange(kv_len, dtype=jnp.int32)
        rel = qpos[:, None] - kpos[None, :]
        bias = rel_table[t5_causal_bucket(rel)].astype(jnp.float32).transpose(2, 0, 1)
        logits = jnp.einsum('bmqd,bmkd->bmqk', q[:, :, q0:kv_len], k[:, :, :kv_len]).astype(jnp.float32) * scale + bias
        logits = jnp.where(rel[None, None] >= 0, logits, -jnp.inf)
        p = jax.nn.softmax(logits, axis=-1).reshape(B, DIFF_HEADS, 2, Q_BLOCK, kv_len)
        a = p[:, :, 0] - lam * p[:, :, 1]
        outs.append(jnp.einsum('bhqk,bhkv->bhqv', a.astype(v.dtype), v[:, :, :kv_len]))
    o = jnp.concatenate(outs, axis=2)
    o = rms_norm(o, subln, SUBLN_EPS) * (1.0 - lam_init)
    return o.transpose(0, 2, 1, 3).reshape(B, S, DIFF_HEADS * DIFF_V_DIM) @ w_out


def moe_swiglu(h, w_router, b_router, w_gate, w_up, w_down):
    B, S, D = h.shape
    xt = h.reshape(B * S, D)
    n_assign = B * S * TOP_K
    logits = (xt @ w_router).astype(jnp.float32) + b_router.astype(jnp.float32)
    top_logit, top_expert = lax.top_k(logits, TOP_K)
    gates = jax.nn.softmax(top_logit, axis=-1)
    expert_flat = top_expert.reshape(-1).astype(jnp.int32)
    gate_flat = gates.reshape(-1)
    order = jnp.argsort(expert_flat * n_assign + jnp.arange(n_assign, dtype=jnp.int32))
    expert_sorted = expert_flat[order]
    token_sorted = (order // TOP_K).astype(jnp.int32)
    counts = jnp.bincount(expert_flat, length=N_EXPERTS)
    padded = (counts + MOE_BLOCK - 1) // MOE_BLOCK * MOE_BLOCK
    pad_end = jnp.cumsum(padded)
    pad_start = pad_end - padded
    raw_start = jnp.cumsum(counts) - counts
    slot = pad_start[expert_sorted] + jnp.arange(n_assign, dtype=jnp.int32) - raw_start[expert_sorted]
    n_blocks = -(-n_assign // MOE_BLOCK) + N_EXPERTS
    n_slots = n_blocks * MOE_BLOCK
    slot_token = jnp.zeros((n_slots,), jnp.int32).at[slot].set(token_sorted)
    slot_gate = jnp.zeros((n_slots,), jnp.float32).at[slot].set(gate_flat[order])
    block_expert = jnp.minimum(
        jnp.searchsorted(pad_end, jnp.arange(n_blocks, dtype=pad_end.dtype) * MOE_BLOCK, side='right'),
        N_EXPERTS - 1)
    xs = xt[slot_token].reshape(n_blocks, MOE_BLOCK, D)

    def expert_block(args):
        xb, e = args
        return swiglu(xb, w_gate[e], w_up[e], w_down[e])

    ys = lax.map(expert_block, (xs, block_expert)).reshape(n_slots, D)
    out = jnp.zeros_like(xt).at[slot_token].add(ys * slot_gate[:, None].astype(ys.dtype))
    return out.reshape(B, S, D)


def setup_inputs(seed: int = 0) -> dict:
    key = jax.random.key(seed)
    ks = iter(jax.random.split(key, 48))

    def nrm(shape, scale):
        return jax.random.normal(next(ks), shape, jnp.float32) * scale

    na, nb, d = N_EVEN_LAYERS, N_ODD_LAYERS, D_MODEL
    return {
        'x': nrm((BATCH, SEQ, d), 1.0),
        'rel_bias_table': nrm((NUM_BUCKETS, 2 * DIFF_HEADS), 0.5),
        'mlstm_norm': 1.0 + nrm((na, d), 0.05),
        'mlstm_w_in': nrm((na, d, MLSTM_PROJ), d ** -0.5),
        'mlstm_conv_w': nrm((na, CONV_WIDTH, MLSTM_QK_WIDTH), CONV_WIDTH ** -0.5),
        'mlstm_conv_b': nrm((na, MLSTM_QK_WIDTH), 0.02),
        'mlstm_b_igate': nrm((na, MLSTM_HEADS), 0.1),
        'mlstm_b_fgate': jnp.linspace(FGATE_BIAS_LO, FGATE_BIAS_HI, MLSTM_HEADS)[None, :] + nrm((na, MLSTM_HEADS), 0.1),
        'mlstm_head_norm': 1.0 + nrm((na, MLSTM_V_WIDTH), 0.05),
        'mlstm_w_out': nrm((na, MLSTM_V_WIDTH, d), MLSTM_V_WIDTH ** -0.5),
        'diff_norm': 1.0 + nrm((nb, d), 0.05),
        'diff_w_in': nrm((nb, d, DIFF_PROJ), d ** -0.5),
        'diff_lambda_q1': nrm((nb, DIFF_HEAD_DIM), 0.1),
        'diff_lambda_k1': nrm((nb, DIFF_HEAD_DIM), 0.1),
        'diff_lambda_q2': nrm((nb, DIFF_HEAD_DIM), 0.1),
        'diff_lambda_k2': nrm((nb, DIFF_HEAD_DIM), 0.1),
        'diff_subln': 1.0 + nrm((nb, DIFF_V_DIM), 0.05),
        'diff_w_out': nrm((nb, DIFF_HEADS * DIFF_V_DIM, d), (DIFF_HEADS * DIFF_V_DIM) ** -0.5),
        'ffn_norm': 1.0 + nrm((na, d), 0.05),
        'ffn_w_gate': nrm((na, d, D_FF), d ** -0.5),
        'ffn_w_up': nrm((na, d, D_FF), d ** -0.5),
        'ffn_w_down': nrm((na, D_FF, d), D_FF ** -0.5),
        'moe_norm': 1.0 + nrm((nb, d), 0.05),
        'moe_w_router': nrm((nb, d, N_EXPERTS), d ** -0.5),
        'moe_b_router': nrm((nb, N_EXPERTS), 0.01),
        'moe_w_gate': nrm((nb, N_EXPERTS, d, D_FF), d ** -0.5),
        'moe_w_up': nrm((nb, N_EXPERTS, d, D_FF), d ** -0.5),
        'moe_w_down': nrm((nb, N_EXPERTS, D_FF, d), D_FF ** -0.5),
        'final_norm': 1.0 + nrm((d,), 0.05),
    }


def reference(x, rel_bias_table,
              mlstm_norm, mlstm_w_in, mlstm_conv_w, mlstm_conv_b, mlstm_b_igate, mlstm_b_fgate,
              mlstm_head_norm, mlstm_w_out,
              diff_norm, diff_w_in, diff_lambda_q1, diff_lambda_k1, diff_lambda_q2, diff_lambda_k2,
              diff_subln, diff_w_out,
              ffn_norm, ffn_w_gate, ffn_w_up, ffn_w_down,
              moe_norm, moe_w_router, moe_b_router, moe_w_gate, moe_w_up, moe_w_down,
              final_norm):
    for i in range(DEPTH):
        s = i // 2
        if i % N_MIXERS == 0:
            x = x + mlstm_mixer(rms_norm(x, mlstm_norm[s]), mlstm_w_in[s], mlstm_conv_w[s], mlstm_conv_b[s],
                                mlstm_b_igate[s], mlstm_b_fgate[s], mlstm_head_norm[s], mlstm_w_out[s])
            x = x + swiglu(rms_norm(x, ffn_norm[s]), ffn_w_gate[s], ffn_w_up[s], ffn_w_down[s])
        else:
            x = x + diff_attention(rms_norm(x, diff_norm[s]), diff_w_in[s], diff_lambda_q1[s], diff_lambda_k1[s],
                                   diff_lambda_q2[s], diff_lambda_k2[s], diff_subln[s], diff_w_out[s],
                                   rel_bias_table, i)
            x = x + moe_swiglu(rms_norm(x, moe_norm[s]), moe_w_router[s], moe_b_router[s],
                               moe_w_gate[s], moe_w_up[s], moe_w_down[s])
    return rms_norm(x, final_norm)
```

```python
import functools
import math

import jax
import jax.numpy as jnp
import numpy as np
from jax import lax
from jax.experimental import pallas as pl
from jax.experimental.pallas import tpu as pltpu

F32 = jnp.float32
BF16 = jnp.bfloat16

MLSTM_HEADS = 4
MLSTM_DQK = 128
MLSTM_DV = 256
CONV_WIDTH = 4
DIFF_HEADS = 8
DIFF_HEAD_DIM = 64
DIFF_V_DIM = 128
NUM_BUCKETS = 32
REL_MAX_DISTANCE = 128
N_EXPERTS = 8
TOP_K = 2
NORM_EPS = 1e-6
SUBLN_EPS = 1e-5

LANES = 128
SUBLANES = 8
VMEM_LIMIT_BYTES = 56 * 1024 * 1024

NEG = -1e30


def _params(*sem):
    return pltpu.CompilerParams(dimension_semantics=sem, vmem_limit_bytes=VMEM_LIMIT_BYTES)


def _rms(x, w, eps):
    return x * lax.rsqrt(jnp.mean(x * x, axis=-1, keepdims=True) + eps) * w


def _sigmoid(x):
    return 1.0 / (1.0 + jnp.exp(-x))


def _dot(a, b):
    return jnp.dot(a, b, preferred_element_type=F32)


def _dot_nt(a, b):
    return lax.dot_general(a, b, (((1,), (1,)), ((), ())), preferred_element_type=F32)


def _mlstm_proj_kernel(x_ref, nw_ref, wqk_ref, wv_ref, wo_ref, wg_ref, cw_ref, cb_ref, gb_ref,
                       q_ref, k_ref, v_ref, og_ref, g_ref, buf_ref, *, tiles_per_seq, tm):
    i = pl.program_id(0)
    h = _rms(x_ref[...], nw_ref[...], NORM_EPS).astype(BF16)

    @pl.when(i % tiles_per_seq == 0)
    def _():
        buf_ref[0:SUBLANES, :] = jnp.zeros((SUBLANES, buf_ref.shape[1]), F32)

    @pl.when(i % tiles_per_seq != 0)
    def _():
        buf_ref[0:SUBLANES, :] = buf_ref[tm:tm + SUBLANES, :]

    buf_ref[SUBLANES:tm + SUBLANES, :] = _dot(h, wqk_ref[...])
    conv = cb_ref[...]
    for j in range(CONV_WIDTH):
        start = SUBLANES - (CONV_WIDTH - 1) + j
        conv = conv + cw_ref[j:j + 1, :] * buf_ref[start:start + tm, :]
    qk = conv * _sigmoid(conv)
    half = qk.shape[1] // 2
    q_ref[...] = qk[:, :half].astype(BF16)
    k_ref[...] = (qk[:, half:] * (MLSTM_DQK ** -0.5)).astype(BF16)
    v_ref[...] = _dot(h, wv_ref[...]).astype(BF16)
    og_ref[...] = _sigmoid(_dot(h, wo_ref[...])).astype(BF16)
    gp = _dot(h, wg_ref[...]) + gb_ref[...]
    lane = lax.broadcasted_iota(jnp.int32, gp.shape, 1)
    log_sig = jnp.minimum(gp, 0.0) - jnp.log1p(jnp.exp(-jnp.abs(gp)))
    g_ref[...] = jnp.where(lane < MLSTM_HEADS, gp, log_sig)


def _mlstm_proj(x, nw, wqk, wv, wo, wg, cw, cb, gb, *, seq, tm):
    n, d = x.shape
    full = lambda r, c: pl.BlockSpec((r, c), lambda i: (0, 0))
    rows = lambda c: pl.BlockSpec((tm, c), lambda i: (i, 0))
    return pl.pallas_call(
        functools.partial(_mlstm_proj_kernel, tiles_per_seq=seq // tm, tm=tm),
        grid=(n // tm,),
        in_specs=[rows(d), full(1, d), full(d, d), full(d, d), full(d, d), full(d, LANES),
                  full(CONV_WIDTH, d), full(1, d), full(1, LANES)],
        out_specs=[rows(d // 2), rows(d // 2), rows(d), rows(d), rows(LANES)],
        out_shape=[jax.ShapeDtypeStruct((n, d // 2), BF16), jax.ShapeDtypeStruct((n, d // 2), BF16),
                   jax.ShapeDtypeStruct((n, d), BF16), jax.ShapeDtypeStruct((n, d), BF16),
                   jax.ShapeDtypeStruct((n, LANES), F32)],
        scratch_shapes=[pltpu.VMEM((tm + SUBLANES, d), F32)],
        compiler_params=_params("arbitrary"),
        name="mlstm_proj",
    )(x, nw, wqk, wv, wo, wg, cw, cb, gb)


def _mlstm_kernel(q_ref, k_ref, v_ref, og_ref, g_ref, hn_ref, o_ref, c_ref, n_ref, m_ref, *, chunk):
    c_idx = pl.program_id(1)

    @pl.when(c_idx == 0)
    def _():
        c_ref[...] = jnp.zeros(c_ref.shape, F32)
        n_ref[...] = jnp.zeros(n_ref.shape, F32)
        m_ref[...] = jnp.zeros(m_ref.shape, F32)

    g = g_ref[...]
    row = lax.broadcasted_iota(jnp.int32, (chunk, chunk), 0)
    col = lax.broadcasted_iota(jnp.int32, (chunk, chunk), 1)
    causal = row >= col
    tri = causal.astype(F32)
    csum = jnp.dot(tri, g, preferred_element_type=F32, precision=lax.Precision.HIGHEST)
    g_t = g.T
    csum_t = csum.T
    for h in range(MLSTM_HEADS):
        q = q_ref[:, h * MLSTM_DQK:(h + 1) * MLSTM_DQK]
        k = k_ref[:, h * MLSTM_DQK:(h + 1) * MLSTM_DQK]
        v = v_ref[:, h * MLSTM_DV:(h + 1) * MLSTM_DV]
        fh = MLSTM_HEADS + h
        b_c = csum[:, fh:fh + 1]
        li_c = g[:, h:h + 1]
        b_r = csum_t[fh:fh + 1, :]
        li_r = g_t[h:h + 1, :]
        m_prev = m_ref[h:h + 1, 0:1]
        log_d = jnp.where(causal, b_c - b_r + li_r, NEG)
        log_inter = b_c + m_prev
        m_t = jnp.maximum(log_inter, jnp.max(log_d, axis=-1, keepdims=True))
        s = _dot_nt(q, k) * jnp.exp(log_d - m_t)
        inter = jnp.exp(log_inter - m_t)
        c_old = c_ref[h]
        n_old = n_ref[h:h + 1, :]
        num = _dot(s.astype(BF16), v) + inter * _dot(q, c_old.astype(BF16))
        den = jnp.sum(s, axis=-1, keepdims=True) + inter * jnp.sum(q.astype(F32) * n_old, axis=-1, keepdims=True)
        hout = num / jnp.maximum(jnp.abs(den), jnp.exp(-m_t))
        g_tot = csum[chunk - 1:chunk, fh:fh + 1]
        m_new = jnp.maximum(g_tot + m_prev, jnp.max(g_tot - b_r + li_r, axis=-1, keepdims=True))
        kw = k.astype(F32) * jnp.exp(g_tot - b_c + li_c - m_new)
        decay = jnp.exp(g_tot + m_prev - m_new)
        c_ref[h] = decay * c_old + _dot(kw.T.astype(BF16), v)
        n_ref[h:h + 1, :] = decay * n_old + jnp.sum(kw, axis=0, keepdims=True)
        m_ref[h:h + 1, :] = jnp.broadcast_to(m_new, (1, LANES))
        hn = _rms(hout, hn_ref[:, h * MLSTM_DV:(h + 1) * MLSTM_DV], NORM_EPS)
        og = og_ref[:, h * MLSTM_DV:(h + 1) * MLSTM_DV].astype(F32)
        o_ref[:, h * MLSTM_DV:(h + 1) * MLSTM_DV] = (og * hn).astype(BF16)


def _mlstm(q, k, v, og, g, hn, *, batch, chunk):
    n, d = v.shape
    nc = n // batch // chunk
    rows = lambda c: pl.BlockSpec((chunk, c), lambda b, t: (b * nc + t, 0))
    return pl.pallas_call(
        functools.partial(_mlstm_kernel, chunk=chunk),
        grid=(batch, nc),
        in_specs=[rows(d // 2), rows(d // 2), rows(d), rows(d), rows(LANES),
                  pl.BlockSpec((1, d), lambda b, t: (0, 0))],
        out_specs=rows(d),
        out_shape=jax.ShapeDtypeStruct((n, d), BF16),
        scratch_shapes=[pltpu.VMEM((MLSTM_HEADS, MLSTM_DQK, MLSTM_DV), F32),
                        pltpu.VMEM((SUBLANES, LANES), F32), pltpu.VMEM((SUBLANES, LANES), F32)],
        compiler_params=_params("arbitrary", "arbitrary"),
        name="mlstm_chunk",
    )(q, k, v, og, g, hn)


def _mm_res_kernel(a_ref, w_ref, r_ref, o_ref):
    o_ref[...] = r_ref[...] + _dot(a_ref[...], w_ref[...])


def _mm_res(a, w, res, *, tm):
    n, kdim = a.shape
    d = w.shape[1]
    return pl.pallas_call(
        _mm_res_kernel,
        grid=(n // tm,),
        in_specs=[pl.BlockSpec((tm, kdim), lambda i: (i, 0)), pl.BlockSpec((kdim, d), lambda i: (0, 0)),
                  pl.BlockSpec((tm, d), lambda i: (i, 0))],
        out_specs=pl.BlockSpec((tm, d), lambda i: (i, 0)),
        out_shape=jax.ShapeDtypeStruct((n, d), F32),
        compiler_params=_params("parallel"),
        name="out_proj_residual",
    )(a, w, res)


def _ffn_kernel(x_ref, nw_ref, wg_ref, wu_ref, wd_ref, o_ref, h_ref):
    j = pl.program_id(1)

    @pl.when(j == 0)
    def _():
        x = x_ref[...]
        h_ref[...] = _rms(x, nw_ref[...], NORM_EPS).astype(BF16)
        o_ref[...] = x

    h = h_ref[...]
    gate = _dot(h, wg_ref[...])
    act = (gate * _sigmoid(gate) * _dot(h, wu_ref[...])).astype(BF16)
    o_ref[...] += _dot(act, wd_ref[...])


def _ffn(x, nw, wg, wu, wd, *, tm, tf):
    n, d = x.shape
    f = wg.shape[1]
    return pl.pallas_call(
        _ffn_kernel,
        grid=(n // tm, f // tf),
        in_specs=[pl.BlockSpec((tm, d), lambda i, j: (i, 0)), pl.BlockSpec((1, d), lambda i, j: (0, 0)),
                  pl.BlockSpec((d, tf), lambda i, j: (0, j)), pl.BlockSpec((d, tf), lambda i, j: (0, j)),
                  pl.BlockSpec((tf, d), lambda i, j: (j, 0))],
        out_specs=pl.BlockSpec((tm, d), lambda i, j: (i, 0)),
        out_shape=jax.ShapeDtypeStruct((n, d), F32),
        scratch_shapes=[pltpu.VMEM((tm, d), BF16)],
        compiler_params=_params("parallel", "arbitrary"),
        name="dense_swiglu",
    )(x, nw, wg, wu, wd)


def _diff_proj_kernel(x_ref, nw_ref, wq_ref, wk_ref, wv_ref, q_ref, k_ref, v_ref):
    h = _rms(x_ref[...], nw_ref[...], NORM_EPS).astype(BF16)
    q_ref[...] = (_dot(h, wq_ref[...]) * (DIFF_HEAD_DIM ** -0.5)).astype(BF16)
    k_ref[...] = _dot(h, wk_ref[...]).astype(BF16)
    v_ref[...] = _dot(h, wv_ref[...]).astype(BF16)


def _diff_proj(x, nw, wq, wk, wv, *, tm):
    n, d = x.shape
    rows = pl.BlockSpec((tm, d), lambda i: (i, 0))
    full = pl.BlockSpec((d, d), lambda i: (0, 0))
    out = jax.ShapeDtypeStruct((n, d), BF16)
    return pl.pallas_call(
        _diff_proj_kernel,
        grid=(n // tm,),
        in_specs=[rows, pl.BlockSpec((1, d), lambda i: (0, 0)), full, full, full],
        out_specs=[rows, rows, rows],
        out_shape=[out, out, out],
        compiler_params=_params("parallel"),
        name="diff_proj",
    )(x, nw, wq, wk, wv)


def _bucket_of_distance(n):
    max_exact = NUM_BUCKETS // 2
    nf = np.maximum(n, 1).astype(np.float32)
    large = max_exact + (np.log(nf / max_exact) / math.log(REL_MAX_DISTANCE / max_exact)
                         * (NUM_BUCKETS - max_exact)).astype(np.int32)
    return np.where(n < max_exact, n, np.minimum(large, NUM_BUCKETS - 1)).astype(np.int32)


def _near_bucket_maps(t):
    qpos = np.arange(t)[:, None]
    kpos = np.arange(t)[None, :]
    diag = qpos - kpos
    sub = diag + t
    return np.stack([np.where(diag >= 0, _bucket_of_distance(np.maximum(diag, 0)), -1),
                     _bucket_of_distance(sub)]).astype(np.int32)


def _bias_tile_kernel(table_ref, map_ref, o_ref):
    m = pl.program_id(0)
    bm = map_ref[0]
    acc = jnp.where(bm < 0, NEG, 0.0).astype(F32)
    for b in range(NUM_BUCKETS):
        acc = jnp.where(bm == b, table_ref[b, m], acc)
    o_ref[0, 0] = acc


def _bias_tiles(rel_table, t):
    n_maps = rel_table.shape[1]
    maps = jnp.asarray(_near_bucket_maps(t))
    return pl.pallas_call(
        _bias_tile_kernel,
        grid=(n_maps, 2),
        in_specs=[pl.BlockSpec(memory_space=pltpu.SMEM), pl.BlockSpec((1, t, t), lambda m, kd: (kd, 0, 0))],
        out_specs=pl.BlockSpec((1, 1, t, t), lambda m, kd: (m, kd, 0, 0)),
        out_shape=jax.ShapeDtypeStruct((n_maps, 2, t, t), F32),
        compiler_params=_params("parallel", "parallel"),
        name="rel_bias_tiles",
    )(rel_table, maps)


def _attn_kernel(far_ref, q_ref, k_ref, v_ref, bias_ref, lam_ref, sub_ref, o_ref,
                 qlo_ref, qhi_ref, m1_ref, l1_ref, a1_ref, m2_ref, l2_ref, a2_ref, *, lam_init):
    h = pl.program_id(1)
    i = pl.program_id(2)
    j = pl.program_id(3)

    @pl.when(j == 0)
    def _():
        q = q_ref[...]
        lane = lax.broadcasted_iota(jnp.int32, q.shape, 1)
        zero = jnp.zeros_like(q)
        qlo_ref[...] = jnp.where(lane < DIFF_HEAD_DIM, q, zero)
        qhi_ref[...] = jnp.where(lane >= DIFF_HEAD_DIM, q, zero)
        for m_ref, l_ref, a_ref in ((m1_ref, l1_ref, a1_ref), (m2_ref, l2_ref, a2_ref)):
            m_ref[...] = jnp.full(m_ref.shape, NEG, F32)
            l_ref[...] = jnp.zeros(l_ref.shape, F32)
            a_ref[...] = jnp.zeros(a_ref.shape, F32)

    def update(bias1, bias2):
        k = k_ref[...]
        v = v_ref[...]
        for qx_ref, bias, m_ref, l_ref, a_ref in ((qlo_ref, bias1, m1_ref, l1_ref, a1_ref),
                                                   (qhi_ref, bias2, m2_ref, l2_ref, a2_ref)):
            s = _dot_nt(qx_ref[...], k) + bias
            m_old = m_ref[...]
            m_new = jnp.maximum(m_old, jnp.max(s, axis=-1, keepdims=True))
            p = jnp.exp(s - m_new)
            alpha = jnp.exp(m_old - m_new)
            l_ref[...] = alpha * l_ref[...] + jnp.sum(p, axis=-1, keepdims=True)
            a_ref[...] = alpha * a_ref[...] + _dot(p.astype(BF16), v)
            m_ref[...] = m_new

    @pl.when(j < i - 1)
    def _():
        update(far_ref[2 * h], far_ref[2 * h + 1])

    @pl.when(j == i - 1)
    def _():
        update(bias_ref[0, 1], bias_ref[1, 1])

    @pl.when(j == i)
    def _():
        update(bias_ref[0, 0], bias_ref[1, 0])
        lam_p = lam_ref[...]
        lam = (jnp.exp(jnp.sum(lam_p[0:1] * lam_p[1:2], axis=-1, keepdims=True))
               - jnp.exp(jnp.sum(lam_p[2:3] * lam_p[3:4], axis=-1, keepdims=True)) + lam_init)
        o = a1_ref[...] / l1_ref[...] - lam * (a2_ref[...] / l2_ref[...])
        o_ref[...] = (_rms(o, sub_ref[...], SUBLN_EPS) * (1.0 - lam_init)).astype(BF16)


def _diff_attn(q, k, v, bias, far, lam_p, subln, *, batch, t, lam_init):
    n, d = q.shape
    nq = n // batch // t
    qspec = pl.BlockSpec((t, LANES), lambda b, h, i, j, far: (b * nq + i, h))
    kspec = pl.BlockSpec((t, LANES), lambda b, h, i, j, far: (b * nq + jnp.minimum(j, i), h))
    grid_spec = pltpu.PrefetchScalarGridSpec(
        num_scalar_prefetch=1,
        grid=(batch, DIFF_HEADS, nq, nq),
        in_specs=[qspec, kspec, kspec,
                  pl.BlockSpec((2, 2, t, t), lambda b, h, i, j, far: (h, 0, 0, 0)),
                  pl.BlockSpec((SUBLANES, LANES), lambda b, h, i, j, far: (0, 0)),
                  pl.BlockSpec((1, LANES), lambda b, h, i, j, far: (0, 0))],
        out_specs=qspec,
        scratch_shapes=[pltpu.VMEM((t, LANES), BF16), pltpu.VMEM((t, LANES), BF16),
                        pltpu.VMEM((t, 1), F32), pltpu.VMEM((t, 1), F32), pltpu.VMEM((t, LANES), F32),
                        pltpu.VMEM((t, 1), F32), pltpu.VMEM((t, 1), F32), pltpu.VMEM((t, LANES), F32)],
    )
    return pl.pallas_call(
        functools.partial(_attn_kernel, lam_init=lam_init),
        grid_spec=grid_spec,
        out_shape=jax.ShapeDtypeStruct((n, d), BF16),
        compiler_params=_params("parallel", "parallel", "parallel", "arbitrary"),
        name="diff_attention",
    )(far, q, k, v, bias, lam_p, subln)


def _router_kernel(x_ref, nw_ref, wr_ref, br_ref, o_ref):
    h = _rms(x_ref[...], nw_ref[...], NORM_EPS)
    logits = jnp.dot(h, wr_ref[...], preferred_element_type=F32, precision=lax.Precision.HIGHEST) + br_ref[...]
    lane = lax.broadcasted_iota(jnp.int32, logits.shape, 1)
    m1 = jnp.max(logits, axis=-1, keepdims=True)
    i1 = jnp.min(jnp.where(logits == m1, lane, LANES), axis=-1, keepdims=True)
    rest = jnp.where(lane == i1, NEG, logits)
    m2 = jnp.max(rest, axis=-1, keepdims=True)
    i2 = jnp.min(jnp.where(rest == m2, lane, LANES), axis=-1, keepdims=True)
    e = jnp.exp(m2 - m1)
    g1 = 1.0 / (1.0 + e)
    g2 = e / (1.0 + e)
    out = jnp.where(lane == 0, i1.astype(F32), 0.0)
    out = jnp.where(lane == 1, i2.astype(F32), out)
    out = jnp.where(lane == 2, g1, out)
    o_ref[...] = jnp.where(lane == 3, g2, out)


def _router(x, nw, wr, br, *, tm):
    n, d = x.shape
    return pl.pallas_call(
        _router_kernel,
        grid=(n // tm,),
        in_specs=[pl.BlockSpec((tm, d), lambda i: (i, 0)), pl.BlockSpec((1, d), lambda i: (0, 0)),
                  pl.BlockSpec((d, LANES), lambda i: (0, 0)), pl.BlockSpec((1, LANES), lambda i: (0, 0))],
        out_specs=pl.BlockSpec((tm, LANES), lambda i: (i, 0)),
        out_shape=jax.ShapeDtypeStruct((n, LANES), F32),
        compiler_params=_params("parallel"),
        name="moe_router",
    )(x, nw, wr, br)


def _moe_kernel(te_ref, nv_ref, tok_ref, tokn_ref, dst_ref, x_hbm, nw_ref, wg_ref, wu_ref, wd_ref, y_hbm,
                xbuf, hbuf, ybuf, gsem, ssem, *, tm, n_tiles):
    i = pl.program_id(0)
    j = pl.program_id(1)
    nj = pl.num_programs(1)
    slot = i % 2
    nvalid = nv_ref[i]
    valid = nvalid > 0

    def gather_row(tok_smem, r, dst_slot):
        return pltpu.make_async_copy(x_hbm.at[pl.ds(tok_smem[0, 0, r], 1), :],
                                     xbuf.at[dst_slot, pl.ds(r, 1), :], gsem.at[dst_slot])

    def scatter_row(r, src_slot):
        return pltpu.make_async_copy(ybuf.at[src_slot, pl.ds(r, 1), :],
                                     y_hbm.at[pl.ds(dst_ref[0, 0, r], 1), :], ssem.at[src_slot])

    def wait_scatter(count, src_slot):
        grouped = pl.multiple_of((count // SUBLANES) * SUBLANES, SUBLANES)

        @pl.when(grouped > 0)
        def _():
            pltpu.make_async_copy(ybuf.at[src_slot, pl.ds(0, grouped), :], y_hbm.at[pl.ds(0, grouped), :],
                                  ssem.at[src_slot]).wait()

        def body(r, c):
            pltpu.make_async_copy(ybuf.at[src_slot, pl.ds(0, 1), :], y_hbm.at[pl.ds(0, 1), :],
                                  ssem.at[src_slot]).wait()
            return c
        lax.fori_loop(0, count - grouped, body, 0)

    @pl.when((j == 0) & (i == 0) & valid)
    def _():
        def body(r, c):
            gather_row(tok_ref, r, 0).start()
            return c
        lax.fori_loop(0, tm, body, 0)

    @pl.when((j == 0) & valid)
    def _():
        pltpu.make_async_copy(x_hbm.at[pl.ds(0, tm), :], xbuf.at[slot], gsem.at[slot]).wait()
        hbuf[...] = _rms(xbuf[slot], nw_ref[...], NORM_EPS).astype(BF16)

    next_valid = nv_ref[jnp.minimum(i + 1, n_tiles - 1)] > 0

    @pl.when((j == 0) & (i + 1 < n_tiles) & next_valid)
    def _():
        def body(r, c):
            gather_row(tokn_ref, r, 1 - slot).start()
            return c
        lax.fori_loop(0, tm, body, 0)

    @pl.when(valid)
    def _():
        h = hbuf[...]
        gate = _dot(h, wg_ref[0])
        act = (gate * _sigmoid(gate) * _dot(h, wu_ref[0])).astype(BF16)
        y = _dot(act, wd_ref[0])

        @pl.when(j == 0)
        def _():
            ybuf[slot] = y

        @pl.when(j != 0)
        def _():
            ybuf[slot] += y

    @pl.when((j == nj - 1) & valid)
    def _():
        @pl.when(i >= 1)
        def _():
            wait_scatter(nv_ref[jnp.maximum(i - 1, 0)], 1 - slot)

        def body(r, c):
            scatter_row(r, slot).start()
            return c
        lax.fori_loop(0, nvalid, body, 0)

        @pl.when((i == n_tiles - 1) | jnp.logical_not(next_valid))
        def _():
            wait_scatter(nvalid, slot)


def _moe_ffn(x, nw, wg, wu, wd, tile_expert, tile_nvalid, slot_token, slot_dst, *, tm, tf):
    n, d = x.shape
    f = wg.shape[2]
    n_tiles = tile_expert.shape[0]
    nj = f // tf
    tok3 = slot_token.reshape(n_tiles, 1, tm)
    dst3 = slot_dst.reshape(n_tiles, 1, tm)

    def jj(i, j, nv):
        return jnp.where(nv[i] > 0, j, nj - 1)

    smem_tile = lambda fn: pl.BlockSpec((1, 1, tm), fn, memory_space=pltpu.SMEM)
    grid_spec = pltpu.PrefetchScalarGridSpec(
        num_scalar_prefetch=2,
        grid=(n_tiles, nj),
        in_specs=[smem_tile(lambda i, j, te, nv: (i, 0, 0)),
                  smem_tile(lambda i, j, te, nv: (jnp.minimum(i + 1, n_tiles - 1), 0, 0)),
                  smem_tile(lambda i, j, te, nv: (i, 0, 0)),
                  pl.BlockSpec(memory_space=pl.ANY),
                  pl.BlockSpec((1, d), lambda i, j, te, nv: (0, 0)),
                  pl.BlockSpec((1, d, tf), lambda i, j, te, nv: (te[i], 0, jj(i, j, nv))),
                  pl.BlockSpec((1, d, tf), lambda i, j, te, nv: (te[i], 0, jj(i, j, nv))),
                  pl.BlockSpec((1, tf, d), lambda i, j, te, nv: (te[i], jj(i, j, nv), 0))],
        out_specs=pl.BlockSpec(memory_space=pl.ANY),
        scratch_shapes=[pltpu.VMEM((2, tm, d), F32), pltpu.VMEM((tm, d), BF16), pltpu.VMEM((2, tm, d), F32),
                        pltpu.SemaphoreType.DMA((2,)), pltpu.SemaphoreType.DMA((2,))],
    )
    return pl.pallas_call(
        functools.partial(_moe_kernel, tm=tm, n_tiles=n_tiles),
        grid_spec=grid_spec,
        out_shape=jax.ShapeDtypeStruct((TOP_K * n, d), F32),
        compiler_params=_params("arbitrary", "arbitrary"),
        name="moe_swiglu",
    )(tile_expert, tile_nvalid, tok3, tok3, dst3, x, nw, wg, wu, wd)


def _combine_kernel(x_ref, y1_ref, y2_ref, r_ref, o_ref):
    r = r_ref[...]
    o_ref[...] = x_ref[...] + r[:, 2:3] * y1_ref[...] + r[:, 3:4] * y2_ref[...]


def _moe_combine(x, y, route, *, tm):
    n, d = x.shape
    nb = n // tm
    rows = pl.BlockSpec((tm, d), lambda i: (i, 0))
    return pl.pallas_call(
        _combine_kernel,
        grid=(nb,),
        in_specs=[rows, rows, pl.BlockSpec((tm, d), lambda i: (i + nb, 0)),
                  pl.BlockSpec((tm, LANES), lambda i: (i, 0))],
        out_specs=rows,
        out_shape=jax.ShapeDtypeStruct((n, d), F32),
        compiler_params=_params("parallel"),
        name="moe_combine",
    )(x, y, y, route)


def _routing_plan(route, *, tm):
    n = route.shape[0]
    n_assign = n * TOP_K
    expert = route[:, :TOP_K].astype(jnp.int32).reshape(-1)
    onehot = (expert[:, None] == jnp.arange(N_EXPERTS, dtype=jnp.int32)[None, :]).astype(jnp.int32)
    csum = jnp.cumsum(onehot, axis=0)
    rank = jnp.sum(jnp.where(onehot > 0, csum, 0), axis=1) - 1
    counts = csum[-1]
    padded = (counts + tm - 1) // tm * tm
    pad_end = jnp.cumsum(padded)
    pad_start = pad_end - padded
    slot = pad_start[expert] + rank
    n_tiles = n_assign // tm + N_EXPERTS
    a = jnp.arange(n_assign, dtype=jnp.int32)
    slot_token = jnp.zeros((n_tiles * tm,), jnp.int32).at[slot].set(a // TOP_K)
    slot_dst = jnp.zeros((n_tiles * tm,), jnp.int32).at[slot].set((a % TOP_K) * n + a // TOP_K)
    tile_start = jnp.arange(n_tiles, dtype=jnp.int32) * tm
    tile_expert = jnp.minimum(jnp.searchsorted(pad_end, tile_start, side='right'), N_EXPERTS - 1).astype(jnp.int32)
    tile_nvalid = jnp.clip(counts[tile_expert] - (tile_start - pad_start[tile_expert]), 0, tm).astype(jnp.int32)
    tile_nvalid = jnp.where(tile_start < pad_end[-1], tile_nvalid, 0)
    last_used = jnp.maximum(pad_end[-1] // tm - 1, 0)
    tile_expert = jnp.where(tile_nvalid > 0, tile_expert, tile_expert[last_used])
    return tile_expert, tile_nvalid, slot_token, slot_dst


def _final_norm_kernel(x_ref, w_ref, o_ref):
    o_ref[...] = _rms(x_ref[...], w_ref[...], NORM_EPS)


def _final_norm(x, w, *, tm):
    n, d = x.shape
    rows = pl.BlockSpec((tm, d), lambda i: (i, 0))
    return pl.pallas_call(
        _final_norm_kernel,
        grid=(n // tm,),
        in_specs=[rows, pl.BlockSpec((1, d), lambda i: (0, 0))],
        out_specs=rows,
        out_shape=jax.ShapeDtypeStruct((n, d), F32),
        compiler_params=_params("parallel"),
        name="final_norm",
    )(x, w)


def _tiles(batch, seq):
    return dict(tm=min(512, seq), chunk=min(256, seq), attn=min(512, seq), ffn_tm=min(1024, seq),
                moe_tm=min(1024, seq), tf=512)


def _row(v):
    return v.reshape(1, -1).astype(F32)


def _pad_lanes(v, fill=0.0):
    v = v.astype(F32)
    return jnp.pad(v, ((0, 0), (0, LANES - v.shape[1])), constant_values=fill)


def _mlstm_layer(x, p, *, batch, seq, tl):
    d = x.shape[1]
    w_in = p['w_in']
    o1, o2, o3 = d, 2 * d, 3 * d
    gb = _pad_lanes(jnp.concatenate([p['b_igate'], p['b_fgate']]).reshape(1, -1))
    q, k, v, og, g = _mlstm_proj(
        x, _row(p['norm']), w_in[:, :o1].astype(BF16), w_in[:, o1:o2].astype(BF16), w_in[:, o2:o3].astype(BF16),
        _pad_lanes(w_in[:, o3:]).astype(BF16), p['conv_w'].astype(F32), _row(p['conv_b']), gb, seq=seq, tm=tl['tm'])
    hh = _mlstm(q, k, v, og, g, _row(p['head_norm']), batch=batch, chunk=tl['chunk'])
    return _mm_res(hh, p['w_out'].astype(BF16), x, tm=tl['tm'])


def _diff_layer(x, p, rel_table, layer_idx, *, batch, seq, tl):
    d = x.shape[1]
    w_in = p['w_in']
    q, k, v = _diff_proj(x, _row(p['norm']), w_in[:, :d].astype(BF16), w_in[:, d:2 * d].astype(BF16),
                         w_in[:, 2 * d:].astype(BF16), tm=tl['tm'])
    rel_table = rel_table.astype(F32)
    bias = _bias_tiles(rel_table, tl['attn'])
    far = rel_table[NUM_BUCKETS - 1]
    lam_p = jnp.zeros((SUBLANES, LANES), F32).at[:4, :DIFF_HEAD_DIM].set(
        jnp.stack([p['lq1'], p['lk1'], p['lq2'], p['lk2']]).astype(F32))
    lam_init = 0.8 - 0.6 * math.exp(-0.3 * layer_idx)
    o = _diff_attn(q, k, v, bias, far, lam_p, _row(p['subln']), batch=batch, t=tl['attn'], lam_init=lam_init)
    return _mm_res(o, p['w_out'].astype(BF16), x, tm=tl['tm'])


def _moe_layer(x, p, *, tl):
    route = _router(x, _row(p['norm']), _pad_lanes(p['w_router']), _pad_lanes(p['b_router'].reshape(1, -1), NEG),
                    tm=tl['tm'])
    plan = _routing_plan(route, tm=tl['moe_tm'])
    y = _moe_ffn(x, _row(p['norm']), p['w_gate'].astype(BF16), p['w_up'].astype(BF16), p['w_down'].astype(BF16),
                 *plan, tm=tl['moe_tm'], tf=tl['tf'])
    return _moe_combine(x, y, route, tm=tl['tm'])


def kernel(x, rel_bias_table, mlstm_norm, mlstm_w_in, mlstm_conv_w, mlstm_conv_b, mlstm_b_igate, mlstm_b_fgate, mlstm_head_norm, mlstm_w_out, diff_norm, diff_w_in, diff_lambda_q1, diff_lambda_k1, diff_lambda_q2, diff_lambda_k2, diff_subln, diff_w_out, ffn_norm, ffn_w_gate, ffn_w_up, ffn_w_down, moe_norm, moe_w_router, moe_b_router, moe_w_gate, moe_w_up, moe_w_down, final_norm):
    batch, seq, d = x.shape
    tl = _tiles(batch, seq)
    depth = mlstm_norm.shape[0] + diff_norm.shape[0]
    xt = x.reshape(batch * seq, d).astype(F32)
    for i in range(depth):
        s = i // 2
        if i % 2 == 0:
            xt = _mlstm_layer(xt, dict(norm=mlstm_norm[s], w_in=mlstm_w_in[s], conv_w=mlstm_conv_w[s],
                                       conv_b=mlstm_conv_b[s], b_igate=mlstm_b_igate[s], b_fgate=mlstm_b_fgate[s],
                                       head_norm=mlstm_head_norm[s], w_out=mlstm_w_out[s]),
                              batch=batch, seq=seq, tl=tl)
            xt = _ffn(xt, _row(ffn_norm[s]), ffn_w_gate[s].astype(BF16), ffn_w_up[s].astype(BF16),
                      ffn_w_down[s].astype(BF16), tm=tl['ffn_tm'], tf=tl['tf'])
        else:
            xt = _diff_layer(xt, dict(norm=diff_norm[s], w_in=diff_w_in[s], lq1=diff_lambda_q1[s],
                                      lk1=diff_lambda_k1[s], lq2=diff_lambda_q2[s], lk2=diff_lambda_k2[s],
                                      subln=diff_subln[s], w_out=diff_w_out[s]),
                             rel_bias_table, i, batch=batch, seq=seq, tl=tl)
            xt = _moe_layer(xt, dict(norm=moe_norm[s], w_router=moe_w_router[s], b_router=moe_b_router[s],
                                     w_gate=moe_w_gate[s], w_up=moe_w_up[s], w_down=moe_w_down[s]), tl=tl)
    return _final_norm(xt, _row(final_norm), tm=tl['tm']).reshape(batch, seq, d)
```

```python
import functools
import math

import jax
import jax.numpy as jnp
import numpy as np
from jax import lax
from jax.experimental import pallas as pl
from jax.experimental.pallas import tpu as pltpu

F32 = jnp.float32
BF16 = jnp.bfloat16

MLSTM_HEADS = 4
MLSTM_DQK = 128
MLSTM_DV = 256
CONV_WIDTH = 4
DIFF_HEADS = 8
DIFF_HEAD_DIM = 64
DIFF_V_DIM = 128
NUM_BUCKETS = 32
REL_MAX_DISTANCE = 128
N_EXPERTS = 8
TOP_K = 2
NORM_EPS = 1e-6
SUBLN_EPS = 1e-5

LANES = 128
SUBLANES = 8
VMEM_LIMIT_BYTES = 56 * 1024 * 1024

NEG = -1e30


def _params(*sem):
    return pltpu.CompilerParams(dimension_semantics=sem, vmem_limit_bytes=VMEM_LIMIT_BYTES)


def _rms(x, w, eps):
    return x * lax.rsqrt(jnp.mean(x * x, axis=-1, keepdims=True) + eps) * w


def _sigmoid(x):
    return 1.0 / (1.0 + jnp.exp(-x))


def _dot(a, b):
    return jnp.dot(a, b, preferred_element_type=F32)


def _dot_nt(a, b):
    return lax.dot_general(a, b, (((1,), (1,)), ((), ())), preferred_element_type=F32)


def _mlstm_proj_kernel(x_ref, nw_ref, wqk_ref, wv_ref, wo_ref, wg_ref, cw_ref, cb_ref, gb_ref,
                       q_ref, k_ref, v_ref, og_ref, g_ref, buf_ref, *, tiles_per_seq, tm):
    i = pl.program_id(0)
    h = _rms(x_ref[...], nw_ref[...], NORM_EPS).astype(BF16)

    @pl.when(i % tiles_per_seq == 0)
    def _():
        buf_ref[0:SUBLANES, :] = jnp.zeros((SUBLANES, buf_ref.shape[1]), F32)

    @pl.when(i % tiles_per_seq != 0)
    def _():
        buf_ref[0:SUBLANES, :] = buf_ref[tm:tm + SUBLANES, :]

    buf_ref[SUBLANES:tm + SUBLANES, :] = _dot(h, wqk_ref[...])
    conv = cb_ref[...]
    for j in range(CONV_WIDTH):
        start = SUBLANES - (CONV_WIDTH - 1) + j
        conv = conv + cw_ref[j:j + 1, :] * buf_ref[start:start + tm, :]
    qk = conv * _sigmoid(conv)
    half = qk.shape[1] // 2
    q_ref[...] = qk[:, :half].astype(BF16)
    k_ref[...] = (qk[:, half:] * (MLSTM_DQK ** -0.5)).astype(BF16)
    v_ref[...] = _dot(h, wv_ref[...]).astype(BF16)
    og_ref[...] = _sigmoid(_dot(h, wo_ref[...])).astype(BF16)
    gp = _dot(h, wg_ref[...]) + gb_ref[...]
    lane = lax.broadcasted_iota(jnp.int32, gp.shape, 1)
    log_sig = jnp.minimum(gp, 0.0) - jnp.log1p(jnp.exp(-jnp.abs(gp)))
    g_ref[...] = jnp.where(lane < MLSTM_HEADS, gp, log_sig)


def _mlstm_proj(x, nw, wqk, wv, wo, wg, cw, cb, gb, *, seq, tm):
    n, d = x.shape
    full = lambda r, c: pl.BlockSpec((r, c), lambda i: (0, 0))
    rows = lambda c: pl.BlockSpec((tm, c), lambda i: (i, 0))
    return pl.pallas_call(
        functools.partial(_mlstm_proj_kernel, tiles_per_seq=seq // tm, tm=tm),
        grid=(n // tm,),
        in_specs=[rows(d), full(1, d), full(d, d), full(d, d), full(d, d), full(d, LANES),
                  full(CONV_WIDTH, d), full(1, d), full(1, LANES)],
        out_specs=[rows(d // 2), rows(d // 2), rows(d), rows(d), rows(LANES)],
        out_shape=[jax.ShapeDtypeStruct((n, d // 2), BF16), jax.ShapeDtypeStruct((n, d // 2), BF16),
                   jax.ShapeDtypeStruct((n, d), BF16), jax.ShapeDtypeStruct((n, d), BF16),
                   jax.ShapeDtypeStruct((n, LANES), F32)],
        scratch_shapes=[pltpu.VMEM((tm + SUBLANES, d), F32)],
        compiler_params=_params("arbitrary"),
        name="mlstm_proj",
    )(x, nw, wqk, wv, wo, wg, cw, cb, gb)


def _mlstm_kernel(q_ref, k_ref, v_ref, og_ref, g_ref, hn_ref, o_ref, c_ref, n_ref, m_ref, *, chunk):
    c_idx = pl.program_id(1)

    @pl.when(c_idx == 0)
    def _():
        c_ref[...] = jnp.zeros(c_ref.shape, F32)
        n_ref[...] = jnp.zeros(n_ref.shape, F32)
        m_ref[...] = jnp.zeros(m_ref.shape, F32)

    g = g_ref[...]
    row = lax.broadcasted_iota(jnp.int32, (chunk, chunk), 0)
    col = lax.broadcasted_iota(jnp.int32, (chunk, chunk), 1)
    causal = row >= col
    tri = causal.astype(F32)
    csum = jnp.dot(tri, g, preferred_element_type=F32, precision=lax.Precision.HIGHEST)
    g_t = g.T
    csum_t = csum.T
    for h in range(MLSTM_HEADS):
        q = q_ref[:, h * MLSTM_DQK:(h + 1) * MLSTM_DQK]
        k = k_ref[:, h * MLSTM_DQK:(h + 1) * MLSTM_DQK]
        v = v_ref[:, h * MLSTM_DV:(h + 1) * MLSTM_DV]
        fh = MLSTM_HEADS + h
        b_c = csum[:, fh:fh + 1]
        li_c = g[:, h:h + 1]
        b_r = csum_t[fh:fh + 1, :]
        li_r = g_t[h:h + 1, :]
        m_prev = m_ref[h:h + 1, 0:1]
        log_d = jnp.where(causal, b_c - b_r + li_r, NEG)
        log_inter = b_c + m_prev
        m_t = jnp.maximum(log_inter, jnp.max(log_d, axis=-1, keepdims=True))
        s = _dot_nt(q, k) * jnp.exp(log_d - m_t)
        inter = jnp.exp(log_inter - m_t)
        c_old = c_ref[h]
        n_old = n_ref[h:h + 1, :]
        num = _dot(s.astype(BF16), v) + inter * _dot(q, c_old.astype(BF16))
        den = jnp.sum(s, axis=-1, keepdims=True) + inter * jnp.sum(q.astype(F32) * n_old, axis=-1, keepdims=True)
        hout = num / jnp.maximum(jnp.abs(den), jnp.exp(-m_t))
        g_tot = csum[chunk - 1:chunk, fh:fh + 1]
        m_new = jnp.maximum(g_tot + m_prev, jnp.max(g_tot - b_r + li_r, axis=-1, keepdims=True))
        kw = k.astype(F32) * jnp.exp(g_tot - b_c + li_c - m_new)
        decay = jnp.exp(g_tot + m_prev - m_new)
        c_ref[h] = decay * c_old + _dot(kw.T.astype(BF16), v)
        n_ref[h:h + 1, :] = decay * n_old + jnp.sum(kw, axis=0, keepdims=True)
        m_ref[h:h + 1, :] = jnp.broadcast_to(m_new, (1, LANES))
        hn = _rms(hout, hn_ref[:, h * MLSTM_DV:(h + 1) * MLSTM_DV], NORM_EPS)
        og = og_ref[:, h * MLSTM_DV:(h + 1) * MLSTM_DV].astype(F32)
        o_ref[:, h * MLSTM_DV:(h + 1) * MLSTM_DV] = (og * hn).astype(BF16)


def _mlstm(q, k, v, og, g, hn, *, batch, chunk):
    n, d = v.shape
    nc = n // batch // chunk
    rows = lambda c: pl.BlockSpec((chunk, c), lambda b, t: (b * nc + t, 0))
    return pl.pallas_call(
        functools.partial(_mlstm_kernel, chunk=chunk),
        grid=(batch, nc),
        in_specs=[rows(d // 2), rows(d // 2), rows(d), rows(d), rows(LANES),
                  pl.BlockSpec((1, d), lambda b, t: (0, 0))],
        out_specs=rows(d),
        out_shape=jax.ShapeDtypeStruct((n, d), BF16),
        scratch_shapes=[pltpu.VMEM((MLSTM_HEADS, MLSTM_DQK, MLSTM_DV), F32),
                        pltpu.VMEM((SUBLANES, LANES), F32), pltpu.VMEM((SUBLANES, LANES), F32)],
        compiler_params=_params("arbitrary", "arbitrary"),
        name="mlstm_chunk",
    )(q, k, v, og, g, hn)


def _mm_res_kernel(a_ref, w_ref, r_ref, o_ref):
    o_ref[...] = r_ref[...] + _dot(a_ref[...], w_ref[...])


def _mm_res(a, w, res, *, tm):
    n, kdim = a.shape
    d = w.shape[1]
    return pl.pallas_call(
        _mm_res_kernel,
        grid=(n // tm,),
        in_specs=[pl.BlockSpec((tm, kdim), lambda i: (i, 0)), pl.BlockSpec((kdim, d), lambda i: (0, 0)),
                  pl.BlockSpec((tm, d), lambda i: (i, 0))],
        out_specs=pl.BlockSpec((tm, d), lambda i: (i, 0)),
        out_shape=jax.ShapeDtypeStruct((n, d), F32),
        compiler_params=_params("parallel"),
        name="out_proj_residual",
    )(a, w, res)


def _ffn_kernel(x_ref, nw_ref, wg_ref, wu_ref, wd_ref, o_ref, h_ref):
    j = pl.program_id(1)

    @pl.when(j == 0)
    def _():
        x = x_ref[...]
        h_ref[...] = _rms(x, nw_ref[...], NORM_EPS).astype(BF16)
        o_ref[...] = x

    h = h_ref[...]
    gate = _dot(h, wg_ref[...])
    act = (gate * _sigmoid(gate) * _dot(h, wu_ref[...])).astype(BF16)
    o_ref[...] += _dot(act, wd_ref[...])


def _ffn(x, nw, wg, wu, wd, *, tm, tf):
    n, d = x.shape
    f = wg.shape[1]
    return pl.pallas_call(
        _ffn_kernel,
        grid=(n // tm, f // tf),
        in_specs=[pl.BlockSpec((tm, d), lambda i, j: (i, 0)), pl.BlockSpec((1, d), lambda i, j: (0, 0)),
                  pl.BlockSpec((d, tf), lambda i, j: (0, j)), pl.BlockSpec((d, tf), lambda i, j: (0, j)),
                  pl.BlockSpec((tf, d), lambda i, j: (j, 0))],
        out_specs=pl.BlockSpec((tm, d), lambda i, j: (i, 0)),
        out_shape=jax.ShapeDtypeStruct((n, d), F32),
        scratch_shapes=[pltpu.VMEM((tm, d), BF16)],
        compiler_params=_params("parallel", "arbitrary"),
        name="dense_swiglu",
    )(x, nw, wg, wu, wd)


def _diff_proj_kernel(x_ref, nw_ref, wq_ref, wk_ref, wv_ref, q_ref, k_ref, v_ref):
    h = _rms(x_ref[...], nw_ref[...], NORM_EPS).astype(BF16)
    q_ref[...] = (_dot(h, wq_ref[...]) * (DIFF_HEAD_DIM ** -0.5)).T.astype(BF16)
    k_ref[...] = _dot(h, wk_ref[...]).astype(BF16)
    v_ref[...] = _dot(h, wv_ref[...]).T.astype(BF16)


def _diff_proj(x, nw, wq, wk, wv, *, tm):
    n, d = x.shape
    rows = pl.BlockSpec((tm, d), lambda i: (i, 0))
    cols = pl.BlockSpec((d, tm), lambda i: (0, i))
    full = pl.BlockSpec((d, d), lambda i: (0, 0))
    return pl.pallas_call(
        _diff_proj_kernel,
        grid=(n // tm,),
        in_specs=[rows, pl.BlockSpec((1, d), lambda i: (0, 0)), full, full, full],
        out_specs=[cols, rows, cols],
        out_shape=[jax.ShapeDtypeStruct((d, n), BF16), jax.ShapeDtypeStruct((n, d), BF16),
                   jax.ShapeDtypeStruct((d, n), BF16)],
        compiler_params=_params("parallel"),
        name="diff_proj",
    )(x, nw, wq, wk, wv)


def _bucket_of_distance(n):
    max_exact = NUM_BUCKETS // 2
    nf = np.maximum(n, 1).astype(np.float32)
    large = max_exact + (np.log(nf / max_exact) / math.log(REL_MAX_DISTANCE / max_exact)
                         * (NUM_BUCKETS - max_exact)).astype(np.int32)
    return np.where(n < max_exact, n, np.minimum(large, NUM_BUCKETS - 1)).astype(np.int32)


def _near_bucket_maps(t):
    qpos = np.arange(t)[None, :]
    kpos = np.arange(t)[:, None]
    diag = qpos - kpos
    sub = diag + t
    return np.stack([np.where(diag >= 0, _bucket_of_distance(np.maximum(diag, 0)), -1),
                     _bucket_of_distance(sub)]).astype(np.int32)


def _bias_tile_kernel(table_ref, map_ref, o_ref):
    m = pl.program_id(0)
    bm = map_ref[0]
    acc = jnp.where(bm < 0, NEG, 0.0).astype(F32)
    for b in range(NUM_BUCKETS):
        acc = jnp.where(bm == b, table_ref[b, m], acc)
    o_ref[0, 0] = acc


def _bias_tiles(rel_table, t):
    n_maps = rel_table.shape[1]
    maps = jnp.asarray(_near_bucket_maps(t))
    return pl.pallas_call(
        _bias_tile_kernel,
        grid=(n_maps, 2),
        in_specs=[pl.BlockSpec(memory_space=pltpu.SMEM), pl.BlockSpec((1, t, t), lambda m, kd: (kd, 0, 0))],
        out_specs=pl.BlockSpec((1, 1, t, t), lambda m, kd: (m, kd, 0, 0)),
        out_shape=jax.ShapeDtypeStruct((n_maps, 2, t, t), F32),
        compiler_params=_params("parallel", "parallel"),
        name="rel_bias_tiles",
    )(rel_table, maps)


def _attn_kernel(qi_ref, kj_ref, far_ref, k_ref, qt_ref, vt_ref, bias_ref, lam_ref, sub_ref, o_ref,
                 qlo_ref, qhi_ref, m1_ref, l1_ref, a1_ref, m2_ref, l2_ref, a2_ref, *, lam_init):
    h = pl.program_id(1)
    step = pl.program_id(2)
    i = qi_ref[step]
    j = kj_ref[step]

    @pl.when(j == 0)
    def _():
        qt = qt_ref[...]
        feat = lax.broadcasted_iota(jnp.int32, qt.shape, 0)
        zero = jnp.zeros_like(qt)
        qlo_ref[...] = jnp.where(feat < DIFF_HEAD_DIM, qt, zero)
        qhi_ref[...] = jnp.where(feat >= DIFF_HEAD_DIM, qt, zero)
        for m_ref, l_ref, a_ref in ((m1_ref, l1_ref, a1_ref), (m2_ref, l2_ref, a2_ref)):
            m_ref[...] = jnp.full(m_ref.shape, NEG, F32)
            l_ref[...] = jnp.zeros(l_ref.shape, F32)
            a_ref[...] = jnp.zeros(a_ref.shape, F32)

    def update(bias1, bias2, uniform_bias):
        k = k_ref[...]
        vt = vt_ref[...]
        for qx_ref, bias, m_ref, l_ref, a_ref in ((qlo_ref, bias1, m1_ref, l1_ref, a1_ref),
                                                   (qhi_ref, bias2, m2_ref, l2_ref, a2_ref)):
            s = _dot(k, qx_ref[...])
            m_old = m_ref[...]
            if uniform_bias:
                m_new = jnp.maximum(m_old, jnp.max(s, axis=0, keepdims=True) + bias)
                p = jnp.exp(s - (m_new - bias))
            else:
                s = s + bias
                m_new = jnp.maximum(m_old, jnp.max(s, axis=0, keepdims=True))
                p = jnp.exp(s - m_new)
            alpha = jnp.exp(m_old - m_new)
            l_ref[...] = alpha * l_ref[...] + jnp.sum(p, axis=0, keepdims=True)
            a_ref[...] = alpha * a_ref[...] + _dot(vt, p.astype(BF16))
            m_ref[...] = m_new

    @pl.when(j < i - 1)
    def _():
        update(far_ref[2 * h], far_ref[2 * h + 1], True)

    @pl.when(j == i - 1)
    def _():
        update(bias_ref[0, 1], bias_ref[1, 1], False)

    @pl.when(j == i)
    def _():
        update(bias_ref[0, 0], bias_ref[1, 0], False)
        lam_p = lam_ref[...]
        lam = (jnp.exp(jnp.sum(lam_p[0:1] * lam_p[1:2], axis=-1, keepdims=True))
               - jnp.exp(jnp.sum(lam_p[2:3] * lam_p[3:4], axis=-1, keepdims=True)) + lam_init)
        o = (a1_ref[...] / l1_ref[...] - lam * (a2_ref[...] / l2_ref[...])).T
        o_ref[...] = (_rms(o, sub_ref[...], SUBLN_EPS) * (1.0 - lam_init)).astype(BF16)


def _diff_attn(qt, k, vt, bias, far, lam_p, subln, *, batch, t, lam_init):
    n, d = k.shape
    nq = n // batch // t
    pairs = [(i, j) for i in range(nq) for j in range(i + 1)]
    qi = jnp.asarray(np.array([p[0] for p in pairs], np.int32))
    kj = jnp.asarray(np.array([p[1] for p in pairs], np.int32))
    tok_q = lambda b, h, s, qi, kj, far: (b * nq + qi[s], h)
    tok_k = lambda b, h, s, qi, kj, far: (b * nq + kj[s], h)
    feat_q = lambda b, h, s, qi, kj, far: (h, b * nq + qi[s])
    feat_k = lambda b, h, s, qi, kj, far: (h, b * nq + kj[s])
    grid_spec = pltpu.PrefetchScalarGridSpec(
        num_scalar_prefetch=3,
        grid=(batch, DIFF_HEADS, len(pairs)),
        in_specs=[pl.BlockSpec((t, LANES), tok_k), pl.BlockSpec((LANES, t), feat_q), pl.BlockSpec((LANES, t), feat_k),
                  pl.BlockSpec((2, 2, t, t), lambda b, h, s, qi, kj, far: (h, 0, 0, 0)),
                  pl.BlockSpec((SUBLANES, LANES), lambda b, h, s, qi, kj, far: (0, 0)),
                  pl.BlockSpec((1, LANES), lambda b, h, s, qi, kj, far: (0, 0))],
        out_specs=pl.BlockSpec((t, LANES), tok_q),
        scratch_shapes=[pltpu.VMEM((LANES, t), BF16), pltpu.VMEM((LANES, t), BF16),
                        pltpu.VMEM((1, t), F32), pltpu.VMEM((1, t), F32), pltpu.VMEM((LANES, t), F32),
                        pltpu.VMEM((1, t), F32), pltpu.VMEM((1, t), F32), pltpu.VMEM((LANES, t), F32)],
    )
    return pl.pallas_call(
        functools.partial(_attn_kernel, lam_init=lam_init),
        grid_spec=grid_spec,
        out_shape=jax.ShapeDtypeStruct((n, d), BF16),
        compiler_params=_params("parallel", "parallel", "arbitrary"),
        name="diff_attention",
    )(qi, kj, far, k, qt, vt, bias, lam_p, subln)


def _router_kernel(x_ref, nw_ref, wr_ref, br_ref, o_ref):
    h = _rms(x_ref[...], nw_ref[...], NORM_EPS)
    logits = jnp.dot(h, wr_ref[...], preferred_element_type=F32, precision=lax.Precision.HIGHEST) + br_ref[...]
    lane = lax.broadcasted_iota(jnp.int32, logits.shape, 1)
    m1 = jnp.max(logits, axis=-1, keepdims=True)
    i1 = jnp.min(jnp.where(logits == m1, lane, LANES), axis=-1, keepdims=True)
    rest = jnp.where(lane == i1, NEG, logits)
    m2 = jnp.max(rest, axis=-1, keepdims=True)
    i2 = jnp.min(jnp.where(rest == m2, lane, LANES), axis=-1, keepdims=True)
    e = jnp.exp(m2 - m1)
    g1 = 1.0 / (1.0 + e)
    g2 = e / (1.0 + e)
    out = jnp.where(lane == 0, i1.astype(F32), 0.0)
    out = jnp.where(lane == 1, i2.astype(F32), out)
    out = jnp.where(lane == 2, g1, out)
    o_ref[...] = jnp.where(lane == 3, g2, out)


def _router(x, nw, wr, br, *, tm):
    n, d = x.shape
    return pl.pallas_call(
        _router_kernel,
        grid=(n // tm,),
        in_specs=[pl.BlockSpec((tm, d), lambda i: (i, 0)), pl.BlockSpec((1, d), lambda i: (0, 0)),
                  pl.BlockSpec((d, LANES), lambda i: (0, 0)), pl.BlockSpec((1, LANES), lambda i: (0, 0))],
        out_specs=pl.BlockSpec((tm, LANES), lambda i: (i, 0)),
        out_shape=jax.ShapeDtypeStruct((n, LANES), F32),
        compiler_params=_params("parallel"),
        name="moe_router",
    )(x, nw, wr, br)


def _moe_kernel(te_ref, nv_ref, cur_ref, nxt_ref, prv_ref, x_hbm, nw_ref, wg_ref, wu_ref, wd_ref, y_hbm,
                xbuf, hbuf, yacc, ybuf, gsem, ssem, *, tm, nj, n_tokens):
    i = pl.program_id(0)
    j = pl.program_id(1)
    slot = i % 2
    other = 1 - slot
    valid = nv_ref[i] > 0
    prev_valid = nv_ref[jnp.maximum(i - 1, 0)] > 0
    fold = SUBLANES
    rows_per_step = tm // nj

    def folded(r):
        return pl.ds(pl.multiple_of(r * fold, fold), fold)

    def gather_row(assign_smem, r, dst_slot):
        token = lax.shift_right_logical(jnp.maximum(assign_smem[0, 0, r], 0), 1)
        return pltpu.make_async_copy(x_hbm.at[folded(token), :], xbuf.at[dst_slot, folded(r), :], gsem.at[dst_slot])

    def scatter_row(r, src_slot, real):
        a = prv_ref[0, 0, r]
        dst = jnp.where((a >= 0) & real, (a & 1) * n_tokens + lax.shift_right_logical(a, 1), TOP_K * n_tokens + r)
        return pltpu.make_async_copy(ybuf.at[src_slot, folded(r), :], y_hbm.at[folded(dst), :], ssem.at[src_slot])

    def wait_gather(dst_slot):
        pltpu.make_async_copy(x_hbm.at[pl.ds(0, tm * fold), :], xbuf.at[dst_slot], gsem.at[dst_slot]).wait()

    def wait_scatter(src_slot):
        pltpu.make_async_copy(ybuf.at[src_slot], y_hbm.at[pl.ds(0, tm * fold), :], ssem.at[src_slot]).wait()

    def issue_neighbours(r):
        gather_row(nxt_ref, r, other).start()
        scatter_row(r, other, i > 0).start()

    @pl.when((j == 0) & (i == 0))
    def _():
        def body(r, c):
            gather_row(cur_ref, r, 0).start()
            return c
        lax.fori_loop(0, tm, body, 0)
        ybuf[1] = jnp.zeros(ybuf.shape[1:], F32)

    @pl.when((j == 0) & valid)
    def _():
        wait_gather(slot)
        x = jnp.concatenate([xbuf[slot, pl.ds(s, tm, stride=fold), :] for s in range(fold)], axis=1)
        hbuf[...] = _rms(x, nw_ref[...], NORM_EPS).astype(BF16)
        for r in range(rows_per_step * nj, tm):
            issue_neighbours(r)

    @pl.when(valid)
    def _():
        base = j * rows_per_step
        for u in range(rows_per_step):
            issue_neighbours(base + u)
        h = hbuf[...]
        gate = _dot(h, wg_ref[0])
        act = (gate * _sigmoid(gate) * _dot(h, wu_ref[0])).astype(BF16)
        y = _dot(act, wd_ref[0])

        @pl.when(j == 0)
        def _():
            yacc[...] = y

        @pl.when((j > 0) & (j < nj - 1))
        def _():
            yacc[...] += y

        @pl.when(j == nj - 1)
        def _():
            @pl.when(i >= 1)
            def _():
                wait_scatter(slot)
            total = yacc[...] + y
            for s in range(fold):
                ybuf[slot, pl.ds(s, tm, stride=fold), :] = total[:, s * LANES:(s + 1) * LANES]

    @pl.when((j == 0) & jnp.logical_not(valid) & prev_valid & (i >= 1))
    def _():
        wait_gather(slot)
        wait_scatter(slot)

        def body(r, c):
            scatter_row(r, other, True).start()
            return c
        lax.fori_loop(0, tm, body, 0)
        wait_scatter(other)


def _moe_ffn(x, nw, wg, wu, wd, tile_expert, tile_nvalid, slot_assign, *, tm, tf):
    n, d = x.shape
    f = wg.shape[2]
    n_tiles = tile_expert.shape[0]
    nj = f // tf
    assert f % tf == 0 and nj >= 2 and d == SUBLANES * LANES
    assign3 = slot_assign.reshape(n_tiles, 1, tm)
    x_folded = x.reshape(n * SUBLANES, LANES)

    def jj(i, j, nv):
        return jnp.where(nv[i] > 0, j, nj - 1)

    smem_tile = lambda fn: pl.BlockSpec((1, 1, tm), fn, memory_space=pltpu.SMEM)
    grid_spec = pltpu.PrefetchScalarGridSpec(
        num_scalar_prefetch=2,
        grid=(n_tiles, nj),
        in_specs=[smem_tile(lambda i, j, te, nv: (i, 0, 0)),
                  smem_tile(lambda i, j, te, nv: (jnp.minimum(i + 1, n_tiles - 1), 0, 0)),
                  smem_tile(lambda i, j, te, nv: (jnp.maximum(i - 1, 0), 0, 0)),
                  pl.BlockSpec(memory_space=pl.ANY),
                  pl.BlockSpec((1, d), lambda i, j, te, nv: (0, 0)),
                  pl.BlockSpec((1, d, tf), lambda i, j, te, nv: (te[i], 0, jj(i, j, nv))),
                  pl.BlockSpec((1, d, tf), lambda i, j, te, nv: (te[i], 0, jj(i, j, nv))),
                  pl.BlockSpec((1, tf, d), lambda i, j, te, nv: (te[i], jj(i, j, nv), 0))],
        out_specs=pl.BlockSpec(memory_space=pl.ANY),
        scratch_shapes=[pltpu.VMEM((2, tm * SUBLANES, LANES), F32), pltpu.VMEM((tm, d), BF16),
                        pltpu.VMEM((tm, d), F32), pltpu.VMEM((2, tm * SUBLANES, LANES), F32),
                        pltpu.SemaphoreType.DMA((2,)), pltpu.SemaphoreType.DMA((2,))],
    )
    return pl.pallas_call(
        functools.partial(_moe_kernel, tm=tm, nj=nj, n_tokens=n),
        grid_spec=grid_spec,
        out_shape=jax.ShapeDtypeStruct(((TOP_K * n + tm) * SUBLANES, LANES), F32),
        compiler_params=_params("arbitrary", "arbitrary"),
        name="moe_swiglu",
    )(tile_expert, tile_nvalid, assign3, assign3, assign3, x_folded, nw, wg, wu, wd)


def _unfold_rows(ref, rows):
    return jnp.concatenate([ref[pl.ds(s, rows, stride=SUBLANES), :] for s in range(SUBLANES)], axis=1)


def _combine_kernel(x_ref, y1_ref, y2_ref, r_ref, o_ref):
    tm = x_ref.shape[0]
    r = r_ref[...]
    o_ref[...] = x_ref[...] + r[:, 2:3] * _unfold_rows(y1_ref, tm) + r[:, 3:4] * _unfold_rows(y2_ref, tm)


def _moe_combine(x, y_folded, route, *, tm):
    n, d = x.shape
    nb = n // tm
    rows = pl.BlockSpec((tm, d), lambda i: (i, 0))
    return pl.pallas_call(
        _combine_kernel,
        grid=(nb,),
        in_specs=[rows, pl.BlockSpec((tm * SUBLANES, LANES), lambda i: (i, 0)),
                  pl.BlockSpec((tm * SUBLANES, LANES), lambda i: (i + nb, 0)),
                  pl.BlockSpec((tm, LANES), lambda i: (i, 0))],
        out_specs=rows,
        out_shape=jax.ShapeDtypeStruct((n, d), F32),
        compiler_params=_params("parallel"),
        name="moe_combine",
    )(x, y_folded, y_folded, route)


def _routing_plan(route, *, tm):
    n = route.shape[0]
    n_assign = n * TOP_K
    expert = route[:, :TOP_K].astype(jnp.int32).reshape(-1)
    onehot = (expert[:, None] == jnp.arange(N_EXPERTS, dtype=jnp.int32)[None, :]).astype(jnp.int32)
    csum = jnp.cumsum(onehot, axis=0)
    rank = jnp.sum(jnp.where(onehot > 0, csum, 0), axis=1) - 1
    counts = csum[-1]
    padded = (counts + tm - 1) // tm * tm
    pad_end = jnp.cumsum(padded)
    pad_start = pad_end - padded
    slot = pad_start[expert] + rank
    n_tiles = (n_assign + N_EXPERTS * (tm - 1)) // tm + 1
    slot_assign = jnp.full((n_tiles * tm,), -1, jnp.int32).at[slot].set(jnp.arange(n_assign, dtype=jnp.int32))
    tile_start = jnp.arange(n_tiles, dtype=jnp.int32) * tm
    tile_expert = jnp.minimum(jnp.searchsorted(pad_end, tile_start, side='right'), N_EXPERTS - 1).astype(jnp.int32)
    tile_nvalid = jnp.clip(counts[tile_expert] - (tile_start - pad_start[tile_expert]), 0, tm).astype(jnp.int32)
    tile_nvalid = jnp.where(tile_start < pad_end[-1], tile_nvalid, 0)
    last_used = jnp.maximum(pad_end[-1] // tm - 1, 0)
    tile_expert = jnp.where(tile_nvalid > 0, tile_expert, tile_expert[last_used])
    return tile_expert, tile_nvalid, slot_assign


def _final_norm_kernel(x_ref, w_ref, o_ref):
    o_ref[...] = _rms(x_ref[...], w_ref[...], NORM_EPS)


def _final_norm(x, w, *, tm):
    n, d = x.shape
    rows = pl.BlockSpec((tm, d), lambda i: (i, 0))
    return pl.pallas_call(
        _final_norm_kernel,
        grid=(n // tm,),
        in_specs=[rows, pl.BlockSpec((1, d), lambda i: (0, 0))],
        out_specs=rows,
        out_shape=jax.ShapeDtypeStruct((n, d), F32),
        compiler_params=_params("parallel"),
        name="final_norm",
    )(x, w)


def _tiles(batch, seq):
    return dict(tm=min(512, seq), chunk=min(256, seq), attn=min(512, seq), ffn_tm=min(1024, seq),
                moe_tm=min(1024, seq), tf=512, moe_tf=512)


def _row(v):
    return v.reshape(1, -1).astype(F32)


def _pad_lanes(v, fill=0.0):
    v = v.astype(F32)
    return jnp.pad(v, ((0, 0), (0, LANES - v.shape[1])), constant_values=fill)


def _mlstm_layer(x, p, *, batch, seq, tl):
    d = x.shape[1]
    w_in = p['w_in']
    o1, o2, o3 = d, 2 * d, 3 * d
    gb = _pad_lanes(jnp.concatenate([p['b_igate'], p['b_fgate']]).reshape(1, -1))
    q, k, v, og, g = _mlstm_proj(
        x, _row(p['norm']), w_in[:, :o1].astype(BF16), w_in[:, o1:o2].astype(BF16), w_in[:, o2:o3].astype(BF16),
        _pad_lanes(w_in[:, o3:]).astype(BF16), p['conv_w'].astype(F32), _row(p['conv_b']), gb, seq=seq, tm=tl['tm'])
    hh = _mlstm(q, k, v, og, g, _row(p['head_norm']), batch=batch, chunk=tl['chunk'])
    return _mm_res(hh, p['w_out'].astype(BF16), x, tm=tl['tm'])


def _diff_layer(x, p, rel_table, layer_idx, *, batch, seq, tl):
    d = x.shape[1]
    w_in = p['w_in']
    qt, k, vt = _diff_proj(x, _row(p['norm']), w_in[:, :d].astype(BF16), w_in[:, d:2 * d].astype(BF16),
                           w_in[:, 2 * d:].astype(BF16), tm=tl['tm'])
    rel_table = rel_table.astype(F32)
    bias = _bias_tiles(rel_table, tl['attn'])
    far = rel_table[NUM_BUCKETS - 1]
    lam_p = jnp.zeros((SUBLANES, LANES), F32).at[:4, :DIFF_HEAD_DIM].set(
        jnp.stack([p['lq1'], p['lk1'], p['lq2'], p['lk2']]).astype(F32))
    lam_init = 0.8 - 0.6 * math.exp(-0.3 * layer_idx)
    o = _diff_attn(qt, k, vt, bias, far, lam_p, _row(p['subln']), batch=batch, t=tl['attn'], lam_init=lam_init)
    return _mm_res(o, p['w_out'].astype(BF16), x, tm=tl['tm'])


def _moe_layer(x, p, *, tl):
    route = _router(x, _row(p['norm']), _pad_lanes(p['w_router']), _pad_lanes(p['b_router'].reshape(1, -1), NEG),
                    tm=tl['tm'])
    plan = _routing_plan(route, tm=tl['moe_tm'])
    y = _moe_ffn(x, _row(p['norm']), p['w_gate'].astype(BF16), p['w_up'].astype(BF16), p['w_down'].astype(BF16),
                 *plan, tm=tl['moe_tm'], tf=tl['moe_tf'])
    return _moe_combine(x, y, route, tm=tl['tm'])


def kernel(x, rel_bias_table, mlstm_norm, mlstm_w_in, mlstm_conv_w, mlstm_conv_b, mlstm_b_igate, mlstm_b_fgate, mlstm_head_norm, mlstm_w_out, diff_norm, diff_w_in, diff_lambda_q1, diff_lambda_k1, diff_lambda_q2, diff_lambda_k2, diff_subln, diff_w_out, ffn_norm, ffn_w_gate, ffn_w_up, ffn_w_down, moe_norm, moe_w_router, moe_b_router, moe_w_gate, moe_w_up, moe_w_down, final_norm):
    batch, seq, d = x.shape
    tl = _tiles(batch, seq)
    depth = mlstm_norm.shape[0] + diff_norm.shape[0]
    xt = x.reshape(batch * seq, d).astype(F32)
    for i in range(depth):
        s = i // 2
        if i % 2 == 0:
            xt = _mlstm_layer(xt, dict(norm=mlstm_norm[s], w_in=mlstm_w_in[s], conv_w=mlstm_conv_w[s],
                                       conv_b=mlstm_conv_b[s], b_igate=mlstm_b_igate[s], b_fgate=mlstm_b_fgate[s],
                                       head_norm=mlstm_head_norm[s], w_out=mlstm_w_out[s]),
                              batch=batch, seq=seq, tl=tl)
            xt = _ffn(xt, _row(ffn_norm[s]), ffn_w_gate[s].astype(BF16), ffn_w_up[s].astype(BF16),
                      ffn_w_down[s].astype(BF16), tm=tl['ffn_tm'], tf=tl['tf'])
        else:
            xt = _diff_layer(xt, dict(norm=diff_norm[s], w_in=diff_w_in[s], lq1=diff_lambda_q1[s],
                                      lk1=diff_lambda_k1[s], lq2=diff_lambda_q2[s], lk2=diff_lambda_k2[s],
                                      subln=diff_subln[s], w_out=diff_w_out[s]),
                             rel_bias_table, i, batch=batch, seq=seq, tl=tl)
            xt = _moe_layer(xt, dict(norm=moe_norm[s], w_router=moe_w_router[s], b_router=moe_b_router[s],
                                     w_gate=moe_w_gate[s], w_up=moe_w_up[s], w_down=moe_w_down[s]), tl=tl)
    return _final_norm(xt, _row(final_norm), tm=tl['tm']).reshape(batch, seq, d)
```
